```python
import jax, jax.numpy as jnp
from jax import lax
import numpy as np

D_MODEL = 2048
BATCH = 4
SEQ = 2048
DEPTH = 1
DEC_BATCH = 128
DEC_SEQ = 1
PAST_LEN = 16384
PAGE_SIZE = 128

GLA_HEADS = 4
GLA_KEY_W = D_MODEL // 2
GLA_VAL_W = D_MODEL
GLA_DK = GLA_KEY_W // GLA_HEADS
GLA_DV = GLA_VAL_W // GLA_HEADS
GLA_RANK = 16
GLA_GATE_NORM = 16.0
GLA_CHUNK = 64
GMLP_W = D_MODEL // 2
GMLP_GROUPS = 4
GMLP_DG = GMLP_W // GMLP_GROUPS
GMLP_CHUNK = 128
N_MEM = 256
XA_HEADS = 4
XA_W = D_MODEL // 2
XA_DH = XA_W // XA_HEADS
N_BRANCH = 3
PEER_HEADS = 8
PEER_KEY_DIM = 256
PEER_HALF = PEER_KEY_DIM // 2
N_KEYS = 128
N_EXPERTS = N_KEYS * N_KEYS
PEER_TOPK = 16
PEER_BLOCK = 128
EPS = 1e-6
IN_SIZES = (GLA_KEY_W, GLA_KEY_W, GLA_VAL_W, GLA_VAL_W, GLA_RANK, GMLP_W, GMLP_W, XA_W, N_BRANCH * D_MODEL)
IN_COLS = GLA_KEY_W * 2 + GLA_VAL_W * 2 + GLA_RANK + GMLP_W * 2 + XA_W + N_BRANCH * D_MODEL

kernel_name = "hybrid_gla_gmlp_peer_decode_step"


def rmsnorm(x, g):
    xf = x.astype(jnp.float32)
    y = xf * lax.rsqrt(jnp.mean(xf * xf, axis=-1, keepdims=True) + EPS) * g.astype(jnp.float32)
    return y.astype(x.dtype)


def layernorm(x, g, b):
    xf = x.astype(jnp.float32)
    mu = jnp.mean(xf, axis=-1, keepdims=True)
    var = jnp.mean(jnp.square(xf - mu), axis=-1, keepdims=True)
    y = (xf - mu) * lax.rsqrt(var + EPS) * g.astype(jnp.float32) + b.astype(jnp.float32)
    return y.astype(x.dtype)


def split_columns(z):
    parts, start = [], 0
    for size in IN_SIZES:
        parts.append(z[..., start:start + size])
        start += size
    return parts


def gla_chunked(q, k, v, log_a, s0):
    B, T = q.shape[0], q.shape[1]
    C = min(GLA_CHUNK, T)
    n = -(-T // C)
    pad = n * C - T

    def prep(a):
        a = jnp.pad(a.astype(jnp.float32), ((0, 0), (0, pad), (0, 0), (0, 0)))
        return a.reshape(B, n, C, GLA_HEADS, a.shape[-1]).transpose(1, 0, 3, 2, 4)

    qc, kc, vc, ac = prep(q), prep(k), prep(v), prep(log_a)
    causal = jnp.tril(jnp.ones((C, C), dtype=bool))[:, :, None]

    def step(s, inp):
        qi, ki, vi, ai = inp
        b = jnp.cumsum(ai, axis=2)
        b_last = b[:, :, -1:, :]
        o_inter = jnp.einsum('bhtk,bhkv->bhtv', qi * jnp.exp(b), s)
        diff = b[:, :, :, None, :] - b[:, :, None, :, :]
        decay = jnp.exp(jnp.where(causal, diff, -jnp.inf))
        scores = jnp.einsum('bhtk,bhtsk,bhsk->bhts', qi, decay, ki)
        o_intra = jnp.einsum('bhts,bhsv->bhtv', scores, vi)
        s_new = s * jnp.exp(b_last[:, :, 0, :, None]) + jnp.einsum('bhsk,bhsv->bhkv', ki * jnp.exp(b_last - b), vi)
        return s_new, o_inter + o_intra

    s_fin, o = lax.scan(step, s0.astype(jnp.float32), (qc, kc, vc, ac))
    o = o.transpose(1, 0, 3, 2, 4).reshape(B, n * C, GLA_HEADS, GLA_DV)[:, :T]
    return o, s_fin


def gmlp_spatial_gate(u, v, ln_g, ln_b, w_s, b_s):
    B, T, _ = v.shape
    C = min(GMLP_CHUNK, T)
    n = T // C
    vn = layernorm(v, ln_g, ln_b)
    vc = vn.reshape(B, n, C, GMLP_GROUPS, GMLP_DG)
    w = jnp.tril(w_s[:, :C, :C]).astype(v.dtype)
    mixed = jnp.einsum('gts,bnsgd->bntgd', w, vc) + b_s[:, :C].T.astype(v.dtype)[None, None, :, :, None]
    return u * mixed.reshape(B, T, GMLP_W), vn


def memory_kv(mem, mem_norm_g, w_mem_kv):
    B = mem.shape[0]
    kv = rmsnorm(mem, mem_norm_g) @ w_mem_kv
    k = kv[..., :XA_W].reshape(B, N_MEM, XA_HEADS, XA_DH)
    v = kv[..., XA_W:].reshape(B, N_MEM, XA_HEADS, XA_DH)
    return k, v


def memory_attention(q, mem_k, mem_v):
    s = jnp.einsum('bthd,bmhd->bhtm', q, mem_k.astype(q.dtype)).astype(jnp.float32) * (XA_DH ** -0.5)
    p = jax.nn.softmax(s, axis=-1).astype(q.dtype)
    return jnp.einsum('bhtm,bmhd->bthd', p, mem_v.astype(q.dtype))


def mixing_block(x, mem_k, mem_v, s0, norm_g, w_in, b_gate, w_alpha_up, b_alpha, gla_norm_g,
                 gmlp_ln_g, gmlp_ln_b, gmlp_w_s, gmlp_b_s, w_br_gla, w_br_gmlp, w_br_xattn, w_out):
    B, T, _ = x.shape
    h = rmsnorm(x, norm_g)
    z = h @ w_in
    q, k, v, r, a_low, gu, gv, xq, gates = split_columns(z)
    log_a = jax.nn.log_sigmoid((a_low @ w_alpha_up + b_alpha).astype(jnp.float32)) / GLA_GATE_NORM
    o, s_new = gla_chunked(q.reshape(B, T, GLA_HEADS, GLA_DK) * (GLA_DK ** -0.5),
                           k.reshape(B, T, GLA_HEADS, GLA_DK),
                           v.reshape(B, T, GLA_HEADS, GLA_DV),
                           log_a.reshape(B, T, GLA_HEADS, GLA_DK), s0)
    o = rmsnorm(o, gla_norm_g.reshape(GLA_HEADS, GLA_DV)).reshape(B, T, GLA_VAL_W)
    branch_a = (o * jax.nn.silu(r.astype(jnp.float32))).astype(x.dtype)
    branch_b, v_rows = gmlp_spatial_gate(jax.nn.gelu(gu, approximate=False), jax.nn.gelu(gv, approximate=False),
                                         gmlp_ln_g, gmlp_ln_b, gmlp_w_s, gmlp_b_s)
    branch_c = memory_attention(xq.reshape(B, T, XA_HEADS, XA_DH), mem_k, mem_v).reshape(B, T, XA_W)
    g = jax.nn.sigmoid((gates + b_gate).astype(jnp.float32)).reshape(B, T, N_BRANCH, D_MODEL).astype(x.dtype)
    merged = g[:, :, 0] * (branch_a @ w_br_gla) + g[:, :, 1] * (branch_b @ w_br_gmlp) + g[:, :, 2] * (branch_c @ w_br_xattn)
    return x + merged @ w_out, s_new, v_rows


def peer_ffn(h, w_q, sub_k1, sub_k2, u_tab, v_tab):
    B, T, D = h.shape
    N = B * T
    hf = h.reshape(N, D)
    q = (hf @ w_q).reshape(N, PEER_HEADS, 2, PEER_HALF).astype(jnp.float32)
    s1 = jnp.einsum('nhd,hkd->nhk', q[:, :, 0], sub_k1.astype(jnp.float32))
    s2 = jnp.einsum('nhd,hkd->nhk', q[:, :, 1], sub_k2.astype(jnp.float32))
    v1, i1 = lax.top_k(s1, PEER_TOPK)
    v2, i2 = lax.top_k(s2, PEER_TOPK)
    cand = (v1[..., :, None] + v2[..., None, :]).reshape(N, PEER_HEADS, PEER_TOPK * PEER_TOPK)
    sv, si = lax.top_k(cand, PEER_TOPK)
    e1 = jnp.take_along_axis(i1, si // PEER_TOPK, axis=-1)
    e2 = jnp.take_along_axis(i2, si % PEER_TOPK, axis=-1)
    idx = e1 * N_KEYS + e2
    gate = jax.nn.softmax(sv, axis=-1)
    n_blk = -(-N // PEER_BLOCK)
    pad = n_blk * PEER_BLOCK - N
    hp = jnp.pad(hf, ((0, pad), (0, 0))).reshape(n_blk, PEER_BLOCK, D)
    ip = jnp.pad(idx, ((0, pad), (0, 0), (0, 0))).reshape(n_blk, PEER_BLOCK, PEER_HEADS, PEER_TOPK)
    gp = jnp.pad(gate, ((0, pad), (0, 0), (0, 0))).reshape(n_blk, PEER_BLOCK, PEER_HEADS, PEER_TOPK)

    def block(args):
        hb, ib, gb = args
        a = jnp.einsum('nhkd,nd->nhk', u_tab[ib], hb)
        coef = (gb * jax.nn.gelu(a.astype(jnp.float32), approximate=False)).astype(hb.dtype)
        return jnp.einsum('nhk,nhkd->nd', coef, v_tab[ib])

    out = lax.map(block, (hp, ip, gp))
    return out.reshape(n_blk * PEER_BLOCK, D)[:N].reshape(B, T, D)


def setup_inputs(seed: int = 0) -> dict:
    key = jax.random.key(seed)
    ks = jax.random.split(key, 40)
    counter = [0]

    def nxt():
        k = ks[counter[0]]
        counter[0] += 1
        return k

    def nrm(shape, scale):
        return jax.random.normal(nxt(), shape, jnp.float32) * scale

    def gain(shape):
        return 1.0 + 0.1 * jax.random.normal(nxt(), shape, jnp.float32)

    return {
        "x_prompt": nrm((BATCH, SEQ, D_MODEL), 1.0),
        "x_sample": nrm((DEC_BATCH, DEC_SEQ, D_MODEL), 1.0),
        "mem_prompt": nrm((BATCH, N_MEM, D_MODEL), 1.0),
        "state_gla": nrm((DEPTH, DEC_BATCH, GLA_HEADS, GLA_DK, GLA_DV), 1.0),
        "cache_mem_k": nrm((DEPTH, DEC_BATCH, N_MEM, XA_HEADS, XA_DH), 1.0),
        "cache_mem_v": nrm((DEPTH, DEC_BATCH, N_MEM, XA_HEADS, XA_DH), 1.0),
        "norm_mix_g": gain((DEPTH, D_MODEL)),
        "w_in": nrm((DEPTH, D_MODEL, IN_COLS), D_MODEL ** -0.5),
        "b_gate": nrm((DEPTH, N_BRANCH * D_MODEL), 0.1),
        "w_alpha_up": nrm((DEPTH, GLA_RANK, GLA_KEY_W), GLA_RANK ** -0.5),
        "b_alpha": nrm((DEPTH, GLA_KEY_W), 0.5),
        "gla_norm_g": gain((DEPTH, GLA_VAL_W)),
        "gmlp_ln_g": gain((DEPTH, GMLP_W)),
        "gmlp_ln_b": nrm((DEPTH, GMLP_W), 0.02),
        "gmlp_w_s": nrm((DEPTH, GMLP_GROUPS, GMLP_CHUNK, GMLP_CHUNK), GMLP_CHUNK ** -0.5),
        "gmlp_b_s": gain((DEPTH, GMLP_GROUPS, GMLP_CHUNK)),
        "mem_norm_g": gain((DEPTH, D_MODEL)),
        "w_mem_kv": nrm((DEPTH, D_MODEL, 2 * XA_W), D_MODEL ** -0.5),
        "w_br_gla": nrm((DEPTH, GLA_VAL_W, D_MODEL), GLA_VAL_W ** -0.5),
        "w_br_gmlp": nrm((DEPTH, GMLP_W, D_MODEL), GMLP_W ** -0.5),
        "w_br_xattn": nrm((DEPTH, XA_W, D_MODEL), XA_W ** -0.5),
        "w_out": nrm((DEPTH, D_MODEL, D_MODEL), D_MODEL ** -0.5),
        "norm_ffn_g": gain((DEPTH, D_MODEL)),
        "peer_w_q": nrm((DEPTH, D_MODEL, PEER_HEADS * PEER_KEY_DIM), D_MODEL ** -0.5),
        "peer_sub_k1": nrm((DEPTH, PEER_HEADS, N_KEYS, PEER_HALF), PEER_HALF ** -0.5),
        "peer_sub_k2": nrm((DEPTH, PEER_HEADS, N_KEYS, PEER_HALF), PEER_HALF ** -0.5),
        "peer_u": nrm((DEPTH, N_EXPERTS, D_MODEL), D_MODEL ** -0.5),
        "peer_v": nrm((DEPTH, N_EXPERTS, D_MODEL), 0.5),
        "norm_final_g": gain((D_MODEL,)),
    }


def reference(x_prompt, x_sample, mem_prompt, state_gla, cache_mem_k, cache_mem_v,
              norm_mix_g, w_in, b_gate, w_alpha_up, b_alpha, gla_norm_g,
              gmlp_ln_g, gmlp_ln_b, gmlp_w_s, gmlp_b_s, mem_norm_g, w_mem_kv,
              w_br_gla, w_br_gmlp, w_br_xattn, w_out, norm_ffn_g,
              peer_w_q, peer_sub_k1, peer_sub_k2, peer_u, peer_v, norm_final_g):
    xp, xs = x_prompt, x_sample
    gla_p, gla_s, mk_p, mv_p, gv_s = [], [], [], [], []
    for l in range(DEPTH):
        block_w = (norm_mix_g[l], w_in[l], b_gate[l], w_alpha_up[l], b_alpha[l], gla_norm_g[l],
                   gmlp_ln_g[l], gmlp_ln_b[l], gmlp_w_s[l], gmlp_b_s[l],
                   w_br_gla[l], w_br_gmlp[l], w_br_xattn[l], w_out[l])
        mk, mv = memory_kv(mem_prompt, mem_norm_g[l], w_mem_kv[l])
        s0_p = jnp.zeros((xp.shape[0], GLA_HEADS, GLA_DK, GLA_DV), jnp.float32)
        xp, sp, _ = mixing_block(xp, mk, mv, s0_p, *block_w)
        xs, ss, vrows = mixing_block(xs, cache_mem_k[l], cache_mem_v[l], state_gla[l], *block_w)
        peer_w = (peer_w_q[l], peer_sub_k1[l], peer_sub_k2[l], peer_u[l], peer_v[l])
        xp = xp + peer_ffn(rmsnorm(xp, norm_ffn_g[l]), *peer_w)
        xs = xs + peer_ffn(rmsnorm(xs, norm_ffn_g[l]), *peer_w)
        gla_p.append(sp)
        gla_s.append(ss)
        mk_p.append(mk)
        mv_p.append(mv)
        gv_s.append(vrows)
    y_prompt = rmsnorm(xp, norm_final_g)
    y_sample = rmsnorm(xs, norm_final_g)
    return (y_prompt, y_sample, jnp.stack(gla_p), jnp.stack(gla_s), jnp.stack(mk_p), jnp.stack(mv_p), jnp.stack(gv_s))
```

```python
import functools
import math

import numpy as np
import jax
import jax.numpy as jnp
from jax import lax
from jax.experimental import pallas as pl
from jax.experimental.pallas import tpu as pltpu

F32 = jnp.float32
BF16 = jnp.bfloat16

D_MODEL = 2048
BATCH = 4
SEQ = 2048
DEC_BATCH = 128
N_PROMPT = BATCH * SEQ
N_TOK = N_PROMPT + DEC_BATCH

GLA_HEADS = 4
GLA_DK = 256
GLA_DV = 512
GLA_RANK = 16
GLA_GATE_NORM = 16.0
GLA_CHUNK = 64
GLA_LEVELS = (32, 16, 8, 4, 2, 1)
GMLP_W = 1024
GMLP_GROUPS = 4
GMLP_DG = 256
GMLP_CHUNK = 128
N_MEM = 256
XA_HEADS = 4
XA_DH = 256
XA_W = 1024
PEER_HEADS = 8
PEER_HALF = 128
N_KEYS = 128
PEER_TOPK = 16
EPS = 1e-6

COL_Q = 0
COL_K = 1024
COL_V = 2048
COL_R = 4096
COL_GATE = 6144
COL_GU = 12288
COL_GV = 13312
COL_XQ = 14336
Z_COLS = 15360
ALOW_PAD = 128

LANE = 128
TOK_TILE = 640
VMEM_LIMIT = 56 * 1024 * 1024


def _cparams(sem):
    return pltpu.CompilerParams(dimension_semantics=sem, vmem_limit_bytes=VMEM_LIMIT)


def _dot(a, b):
    return jnp.dot(a, b, preferred_element_type=F32)


def _dot_nt(a, b):
    return lax.dot_general(a, b, (((1,), (1,)), ((), ())), preferred_element_type=F32)


def _dot_tn(a, b):
    return lax.dot_general(a, b, (((0,), (0,)), ((), ())), preferred_element_type=F32)


def _split3(x):
    hi = x.astype(BF16)
    r1 = x - hi.astype(F32)
    mid = r1.astype(BF16)
    r2 = r1 - mid.astype(F32)
    return hi, mid, r2.astype(BF16)


def _dot_exact_lhs(a, x):
    hi, mid, lo = _split3(x)
    return _dot(a, hi) + _dot(a, mid) + _dot(a, lo)


def _gelu(x):
    return 0.5 * x * (1.0 + lax.erf(x * np.float32(math.sqrt(0.5))))


def _log_sigmoid(x):
    return jnp.minimum(x, 0.0) - jnp.log1p(jnp.exp(-jnp.abs(x)))


def _rms(x, g):
    return x * lax.rsqrt(jnp.mean(x * x, axis=-1, keepdims=True) + EPS) * g


def _norm_matmul_kernel(x_ref, g_ref, w_ref, o_ref, h_scr):
    @pl.when(pl.program_id(1) == 0)
    def _():
        h_scr[...] = _rms(x_ref[...], g_ref[...]).astype(BF16)

    o_ref[...] = _dot(h_scr[...], w_ref[...])


def _norm_matmul_h_kernel(x_ref, g_ref, w_ref, o_ref, h_ref, h_scr):
    @pl.when(pl.program_id(1) == 0)
    def _():
        h = _rms(x_ref[...], g_ref[...]).astype(BF16)
        h_scr[...] = h
        h_ref[...] = h

    o_ref[...] = _dot(h_scr[...], w_ref[...])


def _norm_matmul(x, g, w, tm, tn, with_h=False):
    m, kdim = x.shape
    n = w.shape[1]
    in_specs = [
        pl.BlockSpec((tm, kdim), lambda i, j: (i, 0)),
        pl.BlockSpec((1, kdim), lambda i, j: (0, 0)),
        pl.BlockSpec((kdim, tn), lambda i, j: (0, j)),
    ]
    o_spec = pl.BlockSpec((tm, tn), lambda i, j: (i, j))
    o_shape = jax.ShapeDtypeStruct((m, n), F32)
    if with_h:
        return pl.pallas_call(
            _norm_matmul_h_kernel,
            grid=(m // tm, n // tn),
            in_specs=in_specs,
            out_specs=[o_spec, pl.BlockSpec((tm, kdim), lambda i, j: (i, 0))],
            out_shape=[o_shape, jax.ShapeDtypeStruct((m, kdim), BF16)],
            scratch_shapes=[pltpu.VMEM((tm, kdim), BF16)],
            compiler_params=_cparams(("parallel", "arbitrary")),
            name="norm_matmul_h",
        )(x, g, w)
    return pl.pallas_call(
        _norm_matmul_kernel,
        grid=(m // tm, n // tn),
        in_specs=in_specs,
        out_specs=o_spec,
        out_shape=o_shape,
        scratch_shapes=[pltpu.VMEM((tm, kdim), BF16)],
        compiler_params=_cparams(("parallel", "arbitrary")),
        name="norm_matmul",
    )(x, g, w)


def _gla_prefix_matrix():
    c = GLA_CHUNK
    t = np.arange(c)[:, None]
    s = np.arange(c)[None, :]
    mats = [(s <= t)]
    for m in GLA_LEVELS:
        ref = (t // (2 * m)) * (2 * m) + m
        mats.append(s <= ref)
    return np.concatenate(mats, axis=0).astype(np.float32)


def _gla_log_decay(alow, wup, ba):
    pre = _dot(alow.astype(BF16), wup) + ba
    return _log_sigmoid(pre) * np.float32(1.0 / GLA_GATE_NORM)


def _gla_out(o, gn, r):
    return _rms(o, gn) * (r * jax.nn.sigmoid(r))


def _gla_prompt_kernel(q_ref, k_ref, v_ref, r_ref, al_ref, wup_ref, ba_ref, gn_ref, pm_ref, o_ref, s_ref, st_scr,
                       *, n_chunks):
    t_idx = pl.program_id(2)
    c = GLA_CHUNK

    @pl.when(t_idx == 0)
    def _():
        st_scr[...] = jnp.zeros_like(st_scr)

    row = lax.broadcasted_iota(jnp.int32, (c, c), 0)
    col = lax.broadcasted_iota(jnp.int32, (c, c), 1)
    rowk = lax.broadcasted_iota(jnp.int32, (c, GLA_DK), 0)

    def chunk(ci, carry):
        r0 = pl.multiple_of(ci * c, c)
        q = q_ref[pl.ds(r0, c), :] * np.float32(GLA_DK ** -0.5)
        k = k_ref[pl.ds(r0, c), :]
        v = v_ref[pl.ds(r0, c), :]
        la = _gla_log_decay(al_ref[pl.ds(r0, c), :], wup_ref[...], ba_ref[...])
        pref = _dot_exact_lhs(pm_ref[...], la)
        b = pref[0:c]
        b_last = b[c - 1:c]
        st = st_scr[...]
        o = _dot_nt((q * jnp.exp(b)).astype(BF16), st.astype(BF16))
        scores = jnp.where(row == col, jnp.sum(q * k, axis=-1, keepdims=True), 0.0)
        for li, m in enumerate(GLA_LEVELS):
            beta = pref[(li + 1) * c:(li + 2) * c]
            upper = ((rowk // m) % 2) == 1
            qm = jnp.where(upper, q * jnp.exp(b - beta), 0.0)
            km = jnp.where(upper, 0.0, k * jnp.exp(beta - b))
            sc = _dot_nt(qm.astype(BF16), km.astype(BF16))
            scores = scores + jnp.where((row // (2 * m)) == (col // (2 * m)), sc, 0.0)
        vb = v.astype(BF16)
        o = o + _dot(scores.astype(BF16), vb)
        kd = (k * jnp.exp(b_last - b)).astype(BF16)
        st_scr[...] = st * jnp.exp(b_last) + _dot_tn(vb, kd)
        o_ref[pl.ds(r0, c), :] = _gla_out(o, gn_ref[...], r_ref[pl.ds(r0, c), :]).astype(o_ref.dtype)
        return carry

    lax.fori_loop(0, n_chunks, chunk, 0)

    @pl.when(t_idx == pl.num_programs(2) - 1)
    def _():
        s_ref[0, 0] = st_scr[...].T


def _gla_prompt(z, za, wup, ba, gn, rows_per_step=512):
    n_t = SEQ // rows_per_step
    pm = jnp.asarray(_gla_prefix_matrix(), dtype=BF16)

    def rowblk(b, h, t):
        return b * n_t + t

    kern = functools.partial(_gla_prompt_kernel, n_chunks=rows_per_step // GLA_CHUNK)
    return pl.pallas_call(
        kern,
        grid=(BATCH, GLA_HEADS, n_t),
        in_specs=[
            pl.BlockSpec((rows_per_step, GLA_DK), lambda b, h, t: (rowblk(b, h, t), COL_Q // GLA_DK + h)),
            pl.BlockSpec((rows_per_step, GLA_DK), lambda b, h, t: (rowblk(b, h, t), COL_K // GLA_DK + h)),
            pl.BlockSpec((rows_per_step, GLA_DV), lambda b, h, t: (rowblk(b, h, t), COL_V // GLA_DV + h)),
            pl.BlockSpec((rows_per_step, GLA_DV), lambda b, h, t: (rowblk(b, h, t), COL_R // GLA_DV + h)),
            pl.BlockSpec((rows_per_step, ALOW_PAD), lambda b, h, t: (rowblk(b, h, t), 0)),
            pl.BlockSpec((ALOW_PAD, GLA_DK), lambda b, h, t: (0, h)),
            pl.BlockSpec((1, GLA_DK), lambda b, h, t: (0, h)),
            pl.BlockSpec((1, GLA_DV), lambda b, h, t: (0, h)),
            pl.BlockSpec(pm.shape, lambda b, h, t: (0, 0)),
        ],
        out_specs=[
            pl.BlockSpec((rows_per_step, GLA_DV), lambda b, h, t: (rowblk(b, h, t), h)),
            pl.BlockSpec((1, 1, GLA_DK, GLA_DV), lambda b, h, t: (b, h, 0, 0)),
        ],
        out_shape=[
            jax.ShapeDtypeStruct((N_PROMPT, GLA_HEADS * GLA_DV), BF16),
            jax.ShapeDtypeStruct((BATCH, GLA_HEADS, GLA_DK, GLA_DV), F32),
        ],
        scratch_shapes=[pltpu.VMEM((GLA_DV, GLA_DK), F32)],
        compiler_params=_cparams(("parallel", "parallel", "arbitrary")),
        name="gla_prompt",
    )(z, z, z, z, za, wup, ba, gn, pm)


def _gla_sample_kernel(q_ref, k_ref, v_ref, r_ref, al_ref, wup_ref, ba_ref, gn_ref, s_ref, o_ref, so_ref,
                       at_scr, kt_scr, qt_scr):
    b = pl.program_id(1)

    @pl.when(b == 0)
    def _():
        la = _gla_log_decay(al_ref[...], wup_ref[...], ba_ref[...])
        at_scr[...] = jnp.exp(la).T
        kt_scr[...] = k_ref[...].T
        qt_scr[...] = (q_ref[...] * np.float32(GLA_DK ** -0.5)).T

    lane = lax.broadcasted_iota(jnp.int32, (GLA_DK, DEC_BATCH), 1)

    def column(ref):
        return jnp.sum(jnp.where(lane == b, ref[...], 0.0), axis=-1, keepdims=True)

    v_row = v_ref[pl.ds(b, 1), :]
    s_new = s_ref[0, 0] * column(at_scr) + column(kt_scr) * v_row
    so_ref[0, 0] = s_new
    o = jnp.sum(s_new * column(qt_scr), axis=0, keepdims=True)
    o_ref[pl.ds(b, 1), :] = _gla_out(o, gn_ref[...], r_ref[pl.ds(b, 1), :])


def _gla_sample(z, za, wup, ba, gn, state):
    rb = N_PROMPT // DEC_BATCH
    return pl.pallas_call(
        _gla_sample_kernel,
        grid=(GLA_HEADS, DEC_BATCH),
        in_specs=[
            pl.BlockSpec((DEC_BATCH, GLA_DK), lambda h, b: (rb, COL_Q // GLA_DK + h)),
            pl.BlockSpec((DEC_BATCH, GLA_DK), lambda h, b: (rb, COL_K // GLA_DK + h)),
            pl.BlockSpec((DEC_BATCH, GLA_DV), lambda h, b: (rb, COL_V // GLA_DV + h)),
            pl.BlockSpec((DEC_BATCH, GLA_DV), lambda h, b: (rb, COL_R // GLA_DV + h)),
            pl.BlockSpec((DEC_BATCH, ALOW_PAD), lambda h, b: (rb, 0)),
            pl.BlockSpec((ALOW_PAD, GLA_DK), lambda h, b: (0, h)),
            pl.BlockSpec((1, GLA_DK), lambda h, b: (0, h)),
            pl.BlockSpec((1, GLA_DV), lambda h, b: (0, h)),
            pl.BlockSpec((1, 1, GLA_DK, GLA_DV), lambda h, b: (b, h, 0, 0)),
        ],
        out_specs=[
            pl.BlockSpec((DEC_BATCH, GLA_DV), lambda h, b: (0, h)),
            pl.BlockSpec((1, 1, GLA_DK, GLA_DV), lambda h, b: (b, h, 0, 0)),
        ],
        out_shape=[
            jax.ShapeDtypeStruct((DEC_BATCH, GLA_HEADS * GLA_DV), F32),
            jax.ShapeDtypeStruct((DEC_BATCH, GLA_HEADS, GLA_DK, GLA_DV), F32),
        ],
        scratch_shapes=[pltpu.VMEM((GLA_DK, DEC_BATCH), F32)] * 3,
        compiler_params=_cparams(("arbitrary", "arbitrary")),
        name="gla_sample",
    )(z, z, z, z, za, wup, ba, gn, state)


def _layernorm(v, g, b):
    mu = jnp.mean(v, axis=-1, keepdims=True)
    d = v - mu
    var = jnp.mean(d * d, axis=-1, keepdims=True)
    return d * lax.rsqrt(var + EPS) * g + b


def _gmlp_prompt_kernel(gu_ref, gv_ref, lg_ref, lb_ref, ws_ref, bst_ref, o_ref):
    c = GMLP_CHUNK
    u = _gelu(gu_ref[...])
    vn = _layernorm(_gelu(gv_ref[...]), lg_ref[...], lb_ref[...])
    tril = lax.broadcasted_iota(jnp.int32, (c, c), 0) >= lax.broadcasted_iota(jnp.int32, (c, c), 1)
    for g in range(GMLP_GROUPS):
        sl = slice(g * GMLP_DG, (g + 1) * GMLP_DG)
        w = jnp.where(tril, ws_ref[g], 0.0).astype(BF16)
        mixed = _dot(w, vn[:, sl].astype(BF16)) + bst_ref[:, g:g + 1]
        o_ref[:, sl] = (u[:, sl] * mixed).astype(o_ref.dtype)


def _gmlp_prompt(z, lg, lb, ws, bst):
    c = GMLP_CHUNK
    return pl.pallas_call(
        _gmlp_prompt_kernel,
        grid=(N_PROMPT // c,),
        in_specs=[
            pl.BlockSpec((c, GMLP_W), lambda i: (i, COL_GU // GMLP_W)),
            pl.BlockSpec((c, GMLP_W), lambda i: (i, COL_GV // GMLP_W)),
            pl.BlockSpec((1, GMLP_W), lambda i: (0, 0)),
            pl.BlockSpec((1, GMLP_W), lambda i: (0, 0)),
            pl.BlockSpec((GMLP_GROUPS, c, c), lambda i: (0, 0, 0)),
            pl.BlockSpec((c, GMLP_GROUPS), lambda i: (0, 0)),
        ],
        out_specs=pl.BlockSpec((c, GMLP_W), lambda i: (i, 0)),
        out_shape=jax.ShapeDtypeStruct((N_PROMPT, GMLP_W), BF16),
        compiler_params=_cparams(("parallel",)),
        name="gmlp_prompt",
    )(z, z, lg, lb, ws, bst)


def _gmlp_sample_kernel(gu_ref, gv_ref, lg_ref, lb_ref, wd_ref, bd_ref, o_ref, vn_ref):
    vn = _layernorm(_gelu(gv_ref[...]), lg_ref[...], lb_ref[...])
    vn_ref[...] = vn
    o_ref[...] = (_gelu(gu_ref[...]) * (vn * wd_ref[...] + bd_ref[...])).astype(o_ref.dtype)


def _gmlp_sample(z, lg, lb, wd, bd):
    rb = N_PROMPT // DEC_BATCH
    vec = pl.BlockSpec((1, GMLP_W), lambda i: (0, 0))
    return pl.pallas_call(
        _gmlp_sample_kernel,
        grid=(1,),
        in_specs=[
            pl.BlockSpec((DEC_BATCH, GMLP_W), lambda i: (rb, COL_GU // GMLP_W)),
            pl.BlockSpec((DEC_BATCH, GMLP_W), lambda i: (rb, COL_GV // GMLP_W)),
            vec, vec, vec, vec,
        ],
        out_specs=[pl.BlockSpec((DEC_BATCH, GMLP_W), lambda i: (0, 0))] * 2,
        out_shape=[jax.ShapeDtypeStruct((DEC_BATCH, GMLP_W), BF16), jax.ShapeDtypeStruct((DEC_BATCH, GMLP_W), F32)],
        compiler_params=_cparams(("arbitrary",)),
        name="gmlp_sample",
    )(z, z, lg, lb, wd, bd)


def _softmax_rows(s):
    e = jnp.exp(s - jnp.max(s, axis=-1, keepdims=True))
    return e / jnp.sum(e, axis=-1, keepdims=True)


def _xattn_prompt_kernel(q_ref, k_ref, v_ref, o_ref):
    s = _dot_nt(q_ref[...].astype(BF16), k_ref[...].astype(BF16)) * np.float32(XA_DH ** -0.5)
    o_ref[...] = _dot(_softmax_rows(s).astype(BF16), v_ref[...].astype(BF16)).astype(o_ref.dtype)


def _xattn_prompt(z, kv, tt=512):
    n_t = SEQ // tt
    return pl.pallas_call(
        _xattn_prompt_kernel,
        grid=(BATCH, XA_HEADS, n_t),
        in_specs=[
            pl.BlockSpec((tt, XA_DH), lambda b, h, t: (b * n_t + t, COL_XQ // XA_DH + h)),
            pl.BlockSpec((N_MEM, XA_DH), lambda b, h, t: (b, h)),
            pl.BlockSpec((N_MEM, XA_DH), lambda b, h, t: (b, XA_HEADS + h)),
        ],
        out_specs=pl.BlockSpec((tt, XA_DH), lambda b, h, t: (b * n_t + t, h)),
        out_shape=jax.ShapeDtypeStruct((N_PROMPT, XA_W), BF16),
        compiler_params=_cparams(("parallel", "parallel", "parallel")),
        name="xattn_prompt",
    )(z, kv, kv)


def _xattn_sample_kernel(q_ref, k_ref, v_ref, o_ref):
    b = pl.program_id(0)
    q = q_ref[pl.ds(b, 1), :]
    for h in range(XA_HEADS):
        sl = slice(h * XA_DH, (h + 1) * XA_DH)
        qh = jnp.broadcast_to(q[:, sl], (8, XA_DH)).astype(BF16)
        s = _dot_nt(qh, k_ref[0, :, sl].astype(BF16)) * np.float32(XA_DH ** -0.5)
        oh = _dot(_softmax_rows(s).astype(BF16), v_ref[0, :, sl].astype(BF16))
        o_ref[pl.ds(b, 1), sl] = oh[0:1]


def _xattn_sample(z, ck, cv):
    rb = N_PROMPT // DEC_BATCH
    return pl.pallas_call(
        _xattn_sample_kernel,
        grid=(DEC_BATCH,),
        in_specs=[
            pl.BlockSpec((DEC_BATCH, XA_W), lambda b: (rb, COL_XQ // XA_W)),
            pl.BlockSpec((1, N_MEM, XA_W), lambda b: (b, 0, 0)),
            pl.BlockSpec((1, N_MEM, XA_W), lambda b: (b, 0, 0)),
        ],
        out_specs=pl.BlockSpec((DEC_BATCH, XA_W), lambda b: (0, 0)),
        out_shape=jax.ShapeDtypeStruct((DEC_BATCH, XA_W), F32),
        compiler_params=_cparams(("arbitrary",)),
        name="xattn_sample",
    )(z, ck, cv)


def _merge_kernel(a_ref, b_ref, c_ref, g0_ref, g1_ref, g2_ref, bg0_ref, bg1_ref, bg2_ref, wa_ref, wb_ref, wc_ref,
                  o_ref):
    m = jax.nn.sigmoid(g0_ref[...] + bg0_ref[...]) * _dot(a_ref[...], wa_ref[...])
    m = m + jax.nn.sigmoid(g1_ref[...] + bg1_ref[...]) * _dot(b_ref[...], wb_ref[...])
    m = m + jax.nn.sigmoid(g2_ref[...] + bg2_ref[...]) * _dot(c_ref[...], wc_ref[...])
    o_ref[...] = m.astype(o_ref.dtype)


def _merge(br_a, br_b, br_c, z, bg, wa, wb, wc, tm=TOK_TILE, tn=512):
    gblk = COL_GATE // tn
    per = D_MODEL // tn

    def gate_spec(i):
        return pl.BlockSpec((tm, tn), lambda j, r, i=i: (r, gblk + i * per + j))

    def bias_spec(i):
        return pl.BlockSpec((1, tn), lambda j, r, i=i: (0, i * per + j))

    return pl.pallas_call(
        _merge_kernel,
        grid=(D_MODEL // tn, N_TOK // tm),
        in_specs=[
            pl.BlockSpec((tm, br_a.shape[1]), lambda j, r: (r, 0)),
            pl.BlockSpec((tm, br_b.shape[1]), lambda j, r: (r, 0)),
            pl.BlockSpec((tm, br_c.shape[1]), lambda j, r: (r, 0)),
            gate_spec(0), gate_spec(1), gate_spec(2),
            bias_spec(0), bias_spec(1), bias_spec(2),
            pl.BlockSpec((wa.shape[0], tn), lambda j, r: (0, j)),
            pl.BlockSpec((wb.shape[0], tn), lambda j, r: (0, j)),
            pl.BlockSpec((wc.shape[0], tn), lambda j, r: (0, j)),
        ],
        out_specs=pl.BlockSpec((tm, tn), lambda j, r: (r, j)),
        out_shape=jax.ShapeDtypeStruct((N_TOK, D_MODEL), BF16),
        compiler_params=_cparams(("parallel", "parallel")),
        name="merge",
    )(br_a, br_b, br_c, z, z, z, bg, bg, bg, wa, wb, wc)


def _resid_matmul_kernel(x_ref, m_ref, w_ref, o_ref):
    o_ref[...] = x_ref[...] + _dot(m_ref[...], w_ref[...])


def _resid_matmul(x, m, w, tm=TOK_TILE, tn=512):
    return pl.pallas_call(
        _resid_matmul_kernel,
        grid=(D_MODEL // tn, N_TOK // tm),
        in_specs=[
            pl.BlockSpec((tm, tn), lambda j, r: (r, j)),
            pl.BlockSpec((tm, D_MODEL), lambda j, r: (r, 0)),
            pl.BlockSpec((D_MODEL, tn), lambda j, r: (0, j)),
        ],
        out_specs=pl.BlockSpec((tm, tn), lambda j, r: (r, j)),
        out_shape=jax.ShapeDtypeStruct((N_TOK, D_MODEL), F32),
        compiler_params=_cparams(("parallel", "parallel")),
        name="resid_matmul",
    )(x, m, w)


def _peer_scores_kernel(q_ref, k1_ref, k2_ref, s1_ref, s2_ref):
    q = q_ref[...]
    s1_ref[0] = _dot_nt(k1_ref[0].astype(BF16), q[:, :PEER_HALF].astype(BF16))
    s2_ref[0] = _dot_nt(k2_ref[0].astype(BF16), q[:, PEER_HALF:].astype(BF16))


def _peer_scores(q, k1, k2, tm=TOK_TILE):
    sspec = pl.BlockSpec((1, N_KEYS, tm), lambda h, i: (h, 0, i))
    kspec = pl.BlockSpec((1, N_KEYS, PEER_HALF), lambda h, i: (h, 0, 0))
    sshape = jax.ShapeDtypeStruct((PEER_HEADS, N_KEYS, N_TOK), F32)
    return pl.pallas_call(
        _peer_scores_kernel,
        grid=(PEER_HEADS, N_TOK // tm),
        in_specs=[pl.BlockSpec((tm, 2 * PEER_HALF), lambda h, i: (i, h)), kspec, kspec],
        out_specs=[sspec, sspec],
        out_shape=[sshape, sshape],
        compiler_params=_cparams(("parallel", "parallel")),
        name="peer_scores",
    )(q, k1, k2)


def _top_values(x, count):
    rows = x.shape[0]
    iota = lax.broadcasted_iota(jnp.int32, x.shape, 0).astype(F32)
    vals = []
    for _ in range(count):
        mx = jnp.max(x, axis=0, keepdims=True)
        first = jnp.min(jnp.where(x == mx, iota, np.float32(rows)), axis=0, keepdims=True)
        x = jnp.where(iota == first, -jnp.inf, x)
        vals.append(mx)
    return vals


def _peer_topk_kernel(s1_ref, s2_ref, e1_ref, e2_ref, tau_ref, v2_scr, cand_scr):
    s1 = s1_ref[0]
    s2 = s2_ref[0]
    v1 = _top_values(s1, PEER_TOPK)
    for i, row in enumerate(_top_values(s2, PEER_TOPK)):
        v2_scr[i:i + 1, :] = row
    v2 = v2_scr[...]
    for i in range(PEER_TOPK):
        cand_scr[i * PEER_TOPK:(i + 1) * PEER_TOPK, :] = v1[i] + v2
    top = _top_values(cand_scr[...], PEER_TOPK)
    zsum = top[0] - top[0]
    for c in top:
        zsum = zsum + jnp.exp(c - top[0])
    e1_ref[0] = jnp.exp(s1 - v1[0]) / zsum
    e2_ref[0] = jnp.exp(s2 - v2[0:1, :])
    tau_ref[0] = top[PEER_TOPK - 1]


def _peer_topk(s1t, s2t, tl=LANE):
    sspec = pl.BlockSpec((1, N_KEYS, tl), lambda h, i: (h, 0, i))
    sshape = jax.ShapeDtypeStruct((PEER_HEADS, N_KEYS, N_TOK), F32)
    return pl.pallas_call(
        _peer_topk_kernel,
        grid=(PEER_HEADS, N_TOK // tl),
        in_specs=[sspec, sspec],
        out_specs=[sspec, sspec, pl.BlockSpec((1, 1, tl), lambda h, i: (h, 0, i))],
        out_shape=[sshape, sshape, jax.ShapeDtypeStruct((PEER_HEADS, 1, N_TOK), F32)],
        scratch_shapes=[pltpu.VMEM((PEER_TOPK, tl), F32), pltpu.VMEM((PEER_TOPK * PEER_TOPK, tl), F32)],
        compiler_params=_cparams(("parallel", "parallel")),
        name="peer_topk",
    )(s1t, s2t)


PEER_E1_PER_BLOCK = 4
PEER_EXPERT_BLOCK = PEER_E1_PER_BLOCK * N_KEYS


def _peer_dense_kernel(ht_ref, u_ref, vt_ref, s1_ref, s2_ref, e1_ref, e2_ref, tau_ref, o_ref, acc_scr, coef_scr):
    j = pl.program_id(1)

    @pl.when(j == 0)
    def _():
        acc_scr[...] = jnp.zeros_like(acc_scr)

    act = _gelu(_dot(u_ref[...], ht_ref[...]))
    for l in range(PEER_E1_PER_BLOCK):
        e1 = j * PEER_E1_PER_BLOCK + l
        gate = None
        for h in range(PEER_HEADS):
            pair = s1_ref[h, pl.ds(e1, 1), :] + s2_ref[h]
            w = jnp.where(pair >= tau_ref[h], e2_ref[h] * e1_ref[h, pl.ds(e1, 1), :], 0.0)
            gate = w if gate is None else gate + w
        rows = slice(l * N_KEYS, (l + 1) * N_KEYS)
        coef_scr[rows, :] = (gate * act[rows, :]).astype(BF16)
    acc_scr[...] += _dot(vt_ref[...], coef_scr[...])

    @pl.when(j == pl.num_programs(1) - 1)
    def _():
        o_ref[...] = acc_scr[...].T


def _peer_dense(ht, u, vt, s1t, s2t, e1t, e2t, tau, tn=TOK_TILE):
    eb = PEER_EXPERT_BLOCK
    n_exp = u.shape[0]
    sspec = pl.BlockSpec((PEER_HEADS, N_KEYS, tn), lambda i, j: (0, 0, i))
    return pl.pallas_call(
        _peer_dense_kernel,
        grid=(N_TOK // tn, n_exp // eb),
        in_specs=[
            pl.BlockSpec((D_MODEL, tn), lambda i, j: (0, i)),
            pl.BlockSpec((eb, D_MODEL), lambda i, j: (j, 0)),
            pl.BlockSpec((D_MODEL, eb), lambda i, j: (0, j)),
            sspec, sspec, sspec, sspec,
            pl.BlockSpec((PEER_HEADS, 1, tn), lambda i, j: (0, 0, i)),
        ],
        out_specs=pl.BlockSpec((tn, D_MODEL), lambda i, j: (i, 0)),
        out_shape=jax.ShapeDtypeStruct((N_TOK, D_MODEL), F32),
        scratch_shapes=[pltpu.VMEM((D_MODEL, tn), F32), pltpu.VMEM((eb, tn), BF16)],
        compiler_params=_cparams(("parallel", "arbitrary")),
        name="peer_dense",
    )(ht, u, vt, s1t, s2t, e1t, e2t, tau)


def _final_norm_kernel(x_ref, p_ref, g_ref, o_ref):
    o_ref[...] = _rms(x_ref[...] + p_ref[...], g_ref[...])


def _final_norm(x, p, g, tm=TOK_TILE):
    blk = pl.BlockSpec((tm, D_MODEL), lambda i: (i, 0))
    return pl.pallas_call(
        _final_norm_kernel,
        grid=(N_TOK // tm,),
        in_specs=[blk, blk, pl.BlockSpec((1, D_MODEL), lambda i: (0, 0))],
        out_specs=blk,
        out_shape=jax.ShapeDtypeStruct((N_TOK, D_MODEL), F32),
        compiler_params=_cparams(("parallel",)),
        name="final_norm",
    )(x, p, g)


def kernel(x_prompt, x_sample, mem_prompt, state_gla, cache_mem_k, cache_mem_v, norm_mix_g, w_in, b_gate, w_alpha_up, b_alpha, gla_norm_g, gmlp_ln_g, gmlp_ln_b, gmlp_w_s, gmlp_b_s, mem_norm_g, w_mem_kv, w_br_gla, w_br_gmlp, w_br_xattn, w_out, norm_ffn_g, peer_w_q, peer_sub_k1, peer_sub_k2, peer_u, peer_v, norm_final_g):
    assert w_in.shape[0] == 1, "single layer"
    x = jnp.concatenate([x_prompt.reshape(N_PROMPT, D_MODEL), x_sample.reshape(DEC_BATCH, D_MODEL)], axis=0)

    wi = w_in[0]
    seg = np.cumsum((0, 1024, 1024, 2048, 2048, 16, 1024, 1024, 1024, 6144))
    part = [wi[:, seg[i]:seg[i + 1]] for i in range(9)]
    w_main = jnp.concatenate([part[0], part[1], part[2], part[3], part[8], part[5], part[6], part[7]],
                             axis=1).astype(BF16)
    w_alow = jnp.pad(part[4], ((0, 0), (0, ALOW_PAD - GLA_RANK))).astype(BF16)
    g_mix = norm_mix_g[0].reshape(1, D_MODEL)
    z = _norm_matmul(x, g_mix, w_main, TOK_TILE, 1024)
    za = _norm_matmul(x, g_mix, w_alow, TOK_TILE, ALOW_PAD)

    wup = jnp.pad(w_alpha_up[0], ((0, ALOW_PAD - GLA_RANK), (0, 0))).astype(BF16)
    ba = b_alpha[0].reshape(1, -1)
    gn = gla_norm_g[0].reshape(1, -1)
    br_a_p, gla_state_p = _gla_prompt(z, za, wup, ba, gn)
    br_a_s, gla_state_s = _gla_sample(z, za, wup, ba, gn, state_gla[0])

    lg = gmlp_ln_g[0].reshape(1, -1)
    lb = gmlp_ln_b[0].reshape(1, -1)
    br_b_p = _gmlp_prompt(z, lg, lb, gmlp_w_s[0], gmlp_b_s[0].T)
    wd = jnp.repeat(gmlp_w_s[0][:, 0, 0], GMLP_DG).reshape(1, -1)
    bd = jnp.repeat(gmlp_b_s[0][:, 0], GMLP_DG).reshape(1, -1)
    br_b_s, vn_s = _gmlp_sample(z, lg, lb, wd, bd)

    kv = _norm_matmul(mem_prompt.reshape(BATCH * N_MEM, D_MODEL), mem_norm_g[0].reshape(1, -1),
                      w_mem_kv[0].astype(BF16), 512, 1024)
    br_c_p = _xattn_prompt(z, kv)
    br_c_s = _xattn_sample(z, cache_mem_k[0].reshape(DEC_BATCH, N_MEM, XA_W),
                           cache_mem_v[0].reshape(DEC_BATCH, N_MEM, XA_W))

    br_a = jnp.concatenate([br_a_p, br_a_s.astype(BF16)], axis=0)
    br_b = jnp.concatenate([br_b_p, br_b_s], axis=0)
    br_c = jnp.concatenate([br_c_p, br_c_s.astype(BF16)], axis=0)
    merged = _merge(br_a, br_b, br_c, z, b_gate[0].reshape(1, -1), w_br_gla[0].astype(BF16),
                    w_br_gmlp[0].astype(BF16), w_br_xattn[0].astype(BF16))
    x1 = _resid_matmul(x, merged, w_out[0].astype(BF16))

    q, h2 = _norm_matmul(x1, norm_ffn_g[0].reshape(1, -1), peer_w_q[0].astype(BF16), TOK_TILE, 1024, with_h=True)
    s1t, s2t = _peer_scores(q, peer_sub_k1[0], peer_sub_k2[0])
    e1t, e2t, tau = _peer_topk(s1t, s2t)
    peer = _peer_dense(h2.T, peer_u[0].astype(BF16), peer_v[0].T.astype(BF16), s1t, s2t, e1t, e2t, tau)
    y = _final_norm(x1, peer, norm_final_g.reshape(1, -1))

    y_prompt = y[:N_PROMPT].reshape(BATCH, SEQ, D_MODEL)
    y_sample = y[N_PROMPT:].reshape(DEC_BATCH, 1, D_MODEL)
    mem_k = kv[:, :XA_W].reshape(1, BATCH, N_MEM, XA_HEADS, XA_DH)
    mem_v = kv[:, XA_W:].reshape(1, BATCH, N_MEM, XA_HEADS, XA_DH)
    return (y_prompt, y_sample, gla_state_p[None], gla_state_s[None], mem_k, mem_v,
            vn_s.reshape(1, DEC_BATCH, 1, GMLP_W))
```

```python
import functools
import math

import numpy as np
import jax
import jax.numpy as jnp
from jax import lax
from jax.experimental import pallas as pl
from jax.experimental.pallas import tpu as pltpu

F32 = jnp.float32
BF16 = jnp.bfloat16

D_MODEL = 2048
BATCH = 4
SEQ = 2048
DEC_BATCH = 128
N_PROMPT = BATCH * SEQ
N_TOK = N_PROMPT + DEC_BATCH

GLA_HEADS = 4
GLA_DK = 256
GLA_DV = 512
GLA_RANK = 16
GLA_GATE_NORM = 16.0
GLA_CHUNK = 64
GLA_LEVELS = (32, 16, 8, 4, 2, 1)
GMLP_W = 1024
GMLP_GROUPS = 4
GMLP_DG = 256
GMLP_CHUNK = 128
N_MEM = 256
XA_HEADS = 4
XA_DH = 256
XA_W = 1024
PEER_HEADS = 8
PEER_HALF = 128
N_KEYS = 128
PEER_TOPK = 16
EPS = 1e-6

COL_Q = 0
COL_K = 1024
COL_V = 2048
COL_R = 4096
COL_GU = 6144
COL_GV = 7168
COL_XQ = 8192
COL_GATE = 9216
Z_HEAD_COLS = 6144
Z_TAIL_START = 6144 + GLA_RANK
ALOW_PAD = 128

LANE = 128
TOK_TILE = 768
N_PAD = 11 * TOK_TILE
assert N_PAD >= N_TOK
VMEM_LIMIT = 56 * 1024 * 1024
PEER_VMEM_LIMIT = 60 * 1024 * 1024


def _cparams(sem):
    return pltpu.CompilerParams(dimension_semantics=sem, vmem_limit_bytes=VMEM_LIMIT)


def _dot(a, b):
    return jnp.dot(a, b, preferred_element_type=F32)


def _dot_nt(a, b):
    return lax.dot_general(a, b, (((1,), (1,)), ((), ())), preferred_element_type=F32)


def _dot_tn(a, b):
    return lax.dot_general(a, b, (((0,), (0,)), ((), ())), preferred_element_type=F32)


def _dot_exact_lhs(a, x):
    hi = x.astype(BF16)
    lo = (x - hi.astype(F32)).astype(BF16)
    return _dot(a, hi) + _dot(a, lo)


def _gelu(x):
    return 0.5 * x * (1.0 + lax.erf(x * np.float32(math.sqrt(0.5))))


def _log_sigmoid(x):
    return jnp.minimum(x, 0.0) - jnp.log1p(jnp.exp(-jnp.abs(x)))


def _rms(x, g):
    return x * lax.rsqrt(jnp.mean(x * x, axis=-1, keepdims=True) + EPS) * g


def _norm_matmul_kernel(x_ref, g_ref, w_ref, o_ref, h_scr):
    @pl.when(pl.program_id(1) == 0)
    def _():
        h_scr[...] = _rms(x_ref[...], g_ref[...]).astype(BF16)

    o_ref[...] = _dot(h_scr[...], w_ref[...])


def _norm_matmul_h_kernel(x_ref, g_ref, w_ref, o_ref, ht_ref, h_scr):
    @pl.when(pl.program_id(1) == 0)
    def _():
        h = _rms(x_ref[...], g_ref[...])
        h_scr[...] = h.astype(BF16)
        ht_ref[...] = h.T.astype(BF16)

    o_ref[...] = _dot(h_scr[...], w_ref[...])


def _in_proj_kernel(x_ref, g_ref, wh_ref, wt_ref, wa_ref, z_ref, za_ref, h_scr, *, n_head_tiles):
    j = pl.program_id(1)

    @pl.when(j == 0)
    def _():
        h = _rms(x_ref[...], g_ref[...]).astype(BF16)
        h_scr[...] = h
        za_ref[...] = _dot(h, wa_ref[...])

    @pl.when(j < n_head_tiles)
    def _():
        z_ref[...] = _dot(h_scr[...], wh_ref[...])

    @pl.when(j >= n_head_tiles)
    def _():
        z_ref[...] = _dot(h_scr[...], wt_ref[...])


def _in_proj(x, g, w_head, w_tail, w_alow, tm=TOK_TILE, tn=1024):
    m, kdim = x.shape
    nh = w_head.shape[1] // tn
    nt = w_tail.shape[1] // tn
    return pl.pallas_call(
        functools.partial(_in_proj_kernel, n_head_tiles=nh),
        grid=(m // tm, nh + nt),
        in_specs=[
            pl.BlockSpec((tm, kdim), lambda i, j: (i, 0)),
            pl.BlockSpec((1, kdim), lambda i, j: (0, 0)),
            pl.BlockSpec((kdim, tn), lambda i, j: (0, jnp.minimum(j, nh - 1))),
            pl.BlockSpec((kdim, tn), lambda i, j: (0, jnp.maximum(j - nh, 0))),
            pl.BlockSpec((kdim, ALOW_PAD), lambda i, j: (0, 0)),
        ],
        out_specs=[pl.BlockSpec((tm, tn), lambda i, j: (i, j)), pl.BlockSpec((tm, ALOW_PAD), lambda i, j: (i, 0))],
        out_shape=[jax.ShapeDtypeStruct((m, (nh + nt) * tn), F32), jax.ShapeDtypeStruct((m, ALOW_PAD), F32)],
        scratch_shapes=[pltpu.VMEM((tm, kdim), BF16)],
        compiler_params=_cparams(("parallel", "arbitrary")),
        name="in_proj",
    )(x, g, w_head, w_tail, w_alow)


def _norm_matmul(x, g, w, tm, tn, with_h=False):
    m, kdim = x.shape
    n = w.shape[1]
    in_specs = [
        pl.BlockSpec((tm, kdim), lambda i, j: (i, 0)),
        pl.BlockSpec((1, kdim), lambda i, j: (0, 0)),
        pl.BlockSpec((kdim, tn), lambda i, j: (0, j)),
    ]
    o_spec = pl.BlockSpec((tm, tn), lambda i, j: (i, j))
    o_shape = jax.ShapeDtypeStruct((m, n), F32)
    if with_h:
        return pl.pallas_call(
            _norm_matmul_h_kernel,
            grid=(m // tm, n // tn),
            in_specs=in_specs,
            out_specs=[o_spec, pl.BlockSpec((kdim, tm), lambda i, j: (0, i))],
            out_shape=[o_shape, jax.ShapeDtypeStruct((kdim, m), BF16)],
            scratch_shapes=[pltpu.VMEM((tm, kdim), BF16)],
            compiler_params=_cparams(("parallel", "arbitrary")),
            name="norm_matmul_h",
        )(x, g, w)
    return pl.pallas_call(
        _norm_matmul_kernel,
        grid=(m // tm, n // tn),
        in_specs=in_specs,
        out_specs=o_spec,
        out_shape=o_shape,
        scratch_shapes=[pltpu.VMEM((tm, kdim), BF16)],
        compiler_params=_cparams(("parallel", "arbitrary")),
        name="norm_matmul",
    )(x, g, w)


def _gla_prefix_matrix():
    c = GLA_CHUNK
    t = np.arange(c)[:, None]
    s = np.arange(c)[None, :]
    mats = [(s <= t)]
    for m in GLA_LEVELS:
        ref = (t // (2 * m)) * (2 * m) + m
        mats.append(s <= ref)
    return np.concatenate(mats, axis=0).astype(np.float32)


def _gla_log_decay(alow, wup, ba):
    pre = _dot(alow.astype(BF16), wup) + ba
    return _log_sigmoid(pre) * np.float32(1.0 / GLA_GATE_NORM)


def _gla_out(o, gn, r):
    return _rms(o, gn) * (r * jax.nn.sigmoid(r))


def _gla_prompt_kernel(q_ref, k_ref, v_ref, r_ref, al_ref, wup_ref, ba_ref, gn_ref, pm_ref, o_ref, s_ref, st_scr,
                       *, n_chunks):
    t_idx = pl.program_id(1)
    c = GLA_CHUNK

    @pl.when(t_idx == 0)
    def _():
        st_scr[...] = jnp.zeros_like(st_scr)

    row = lax.broadcasted_iota(jnp.int32, (c, c), 0)
    col = lax.broadcasted_iota(jnp.int32, (c, c), 1)
    rowk = lax.broadcasted_iota(jnp.int32, (c, GLA_DK), 0)

    def chunk_head(r0, h):
        ks = slice(h * GLA_DK, (h + 1) * GLA_DK)
        vs = slice(h * GLA_DV, (h + 1) * GLA_DV)
        q = q_ref[pl.ds(r0, c), ks] * np.float32(GLA_DK ** -0.5)
        k = k_ref[pl.ds(r0, c), ks]
        v = v_ref[pl.ds(r0, c), vs]
        la = _gla_log_decay(al_ref[pl.ds(r0, c), :], wup_ref[:, ks], ba_ref[:, ks])
        pref = _dot_exact_lhs(pm_ref[...], la)
        b = pref[0:c]
        b_last = b[c - 1:c]
        st = st_scr[h]
        o = _dot_nt((q * jnp.exp(b)).astype(BF16), st.astype(BF16))
        scores = jnp.where(row == col, jnp.sum(q * k, axis=-1, keepdims=True), 0.0)
        for li, m in enumerate(GLA_LEVELS):
            beta = pref[(li + 1) * c:(li + 2) * c]
            upper = ((rowk // m) % 2) == 1
            qm = jnp.where(upper, q * jnp.exp(b - beta), 0.0)
            km = jnp.where(upper, 0.0, k * jnp.exp(beta - b))
            sc = _dot_nt(qm.astype(BF16), km.astype(BF16))
            scores = scores + jnp.where((row // (2 * m)) == (col // (2 * m)), sc, 0.0)
        vb = v.astype(BF16)
        o = o + _dot(scores.astype(BF16), vb)
        kd = (k * jnp.exp(b_last - b)).astype(BF16)
        st_scr[h] = st * jnp.exp(b_last) + _dot_tn(vb, kd)
        o_ref[pl.ds(r0, c), vs] = _gla_out(o, gn_ref[:, vs], r_ref[pl.ds(r0, c), vs]).astype(o_ref.dtype)

    def chunk(ci, carry):
        r0 = pl.multiple_of(ci * c, c)
        for h in range(GLA_HEADS):
            chunk_head(r0, h)
        return carry

    lax.fori_loop(0, n_chunks, chunk, 0)

    @pl.when(t_idx == pl.num_programs(1) - 1)
    def _():
        for h in range(GLA_HEADS):
            s_ref[0, h] = st_scr[h].T


def _gla_prompt(z, za, wup, ba, gn, rows_per_step=256):
    n_t = SEQ // rows_per_step
    pm = jnp.asarray(_gla_prefix_matrix(), dtype=BF16)
    kw = GLA_HEADS * GLA_DK
    vw = GLA_HEADS * GLA_DV
    kern = functools.partial(_gla_prompt_kernel, n_chunks=rows_per_step // GLA_CHUNK)
    return pl.pallas_call(
        kern,
        grid=(BATCH, n_t),
        in_specs=[
            pl.BlockSpec((rows_per_step, kw), lambda b, t: (b * n_t + t, COL_Q // kw)),
            pl.BlockSpec((rows_per_step, kw), lambda b, t: (b * n_t + t, COL_K // kw)),
            pl.BlockSpec((rows_per_step, vw), lambda b, t: (b * n_t + t, COL_V // vw)),
            pl.BlockSpec((rows_per_step, vw), lambda b, t: (b * n_t + t, COL_R // vw)),
            pl.BlockSpec((rows_per_step, ALOW_PAD), lambda b, t: (b * n_t + t, 0)),
            pl.BlockSpec((ALOW_PAD, kw), lambda b, t: (0, 0)),
            pl.BlockSpec((1, kw), lambda b, t: (0, 0)),
            pl.BlockSpec((1, vw), lambda b, t: (0, 0)),
            pl.BlockSpec(pm.shape, lambda b, t: (0, 0)),
        ],
        out_specs=[
            pl.BlockSpec((rows_per_step, vw), lambda b, t: (b * n_t + t, 0)),
            pl.BlockSpec((1, GLA_HEADS, GLA_DK, GLA_DV), lambda b, t: (b, 0, 0, 0)),
        ],
        out_shape=[
            jax.ShapeDtypeStruct((N_PROMPT, vw), BF16),
            jax.ShapeDtypeStruct((BATCH, GLA_HEADS, GLA_DK, GLA_DV), F32),
        ],
        scratch_shapes=[pltpu.VMEM((GLA_HEADS, GLA_DV, GLA_DK), F32)],
        compiler_params=_cparams(("parallel", "arbitrary")),
        name="gla_prompt",
    )(z, z, z, z, za, wup, ba, gn, pm)


def _gla_sample_kernel(q_ref, k_ref, v_ref, r_ref, al_ref, wup_ref, ba_ref, gn_ref, s_ref, o_ref, so_ref,
                       at_scr, kt_scr, qt_scr):
    b = pl.program_id(1)

    @pl.when(b == 0)
    def _():
        la = _gla_log_decay(al_ref[...], wup_ref[...], ba_ref[...])
        at_scr[...] = jnp.exp(la).T
        kt_scr[...] = k_ref[...].T
        qt_scr[...] = (q_ref[...] * np.float32(GLA_DK ** -0.5)).T

    lane = lax.broadcasted_iota(jnp.int32, (GLA_DK, DEC_BATCH), 1)

    def column(ref):
        return jnp.sum(jnp.where(lane == b, ref[...], 0.0), axis=-1, keepdims=True)

    v_row = v_ref[pl.ds(b, 1), :]
    s_new = s_ref[0, 0] * column(at_scr) + column(kt_scr) * v_row
    so_ref[0, 0] = s_new
    o = jnp.sum(s_new * column(qt_scr), axis=0, keepdims=True)
    o_ref[pl.ds(b, 1), :] = _gla_out(o, gn_ref[...], r_ref[pl.ds(b, 1), :])


def _gla_sample(z, za, wup, ba, gn, state):
    rb = N_PROMPT // DEC_BATCH
    return pl.pallas_call(
        _gla_sample_kernel,
        grid=(GLA_HEADS, DEC_BATCH),
        in_specs=[
            pl.BlockSpec((DEC_BATCH, GLA_DK), lambda h, b: (rb, COL_Q // GLA_DK + h)),
            pl.BlockSpec((DEC_BATCH, GLA_DK), lambda h, b: (rb, COL_K // GLA_DK + h)),
            pl.BlockSpec((DEC_BATCH, GLA_DV), lambda h, b: (rb, COL_V // GLA_DV + h)),
            pl.BlockSpec((DEC_BATCH, GLA_DV), lambda h, b: (rb, COL_R // GLA_DV + h)),
            pl.BlockSpec((DEC_BATCH, ALOW_PAD), lambda h, b: (rb, 0)),
            pl.BlockSpec((ALOW_PAD, GLA_DK), lambda h, b: (0, h)),
            pl.BlockSpec((1, GLA_DK), lambda h, b: (0, h)),
            pl.BlockSpec((1, GLA_DV), lambda h, b: (0, h)),
            pl.BlockSpec((1, 1, GLA_DK, GLA_DV), lambda h, b: (b, h, 0, 0)),
        ],
        out_specs=[
            pl.BlockSpec((DEC_BATCH, GLA_DV), lambda h, b: (0, h)),
            pl.BlockSpec((1, 1, GLA_DK, GLA_DV), lambda h, b: (b, h, 0, 0)),
        ],
        out_shape=[
            jax.ShapeDtypeStruct((DEC_BATCH, GLA_HEADS * GLA_DV), F32),
            jax.ShapeDtypeStruct((DEC_BATCH, GLA_HEADS, GLA_DK, GLA_DV), F32),
        ],
        scratch_shapes=[pltpu.VMEM((GLA_DK, DEC_BATCH), F32)] * 3,
        compiler_params=_cparams(("arbitrary", "arbitrary")),
        name="gla_sample",
    )(z, z, z, z, za, wup, ba, gn, state)


def _layernorm(v, g, b):
    mu = jnp.mean(v, axis=-1, keepdims=True)
    d = v - mu
    var = jnp.mean(d * d, axis=-1, keepdims=True)
    return d * lax.rsqrt(var + EPS) * g + b


def _gmlp_prompt_kernel(gu_ref, gv_ref, lg_ref, lb_ref, ws_ref, bst_ref, o_ref):
    c = GMLP_CHUNK
    u = _gelu(gu_ref[...])
    vn = _layernorm(_gelu(gv_ref[...]), lg_ref[...], lb_ref[...])
    tril = lax.broadcasted_iota(jnp.int32, (c, c), 0) >= lax.broadcasted_iota(jnp.int32, (c, c), 1)
    for g in range(GMLP_GROUPS):
        sl = slice(g * GMLP_DG, (g + 1) * GMLP_DG)
        w = jnp.where(tril, ws_ref[g], 0.0).astype(BF16)
        mixed = _dot(w, vn[:, sl].astype(BF16)) + bst_ref[:, g:g + 1]
        o_ref[:, sl] = (u[:, sl] * mixed).astype(o_ref.dtype)


def _gmlp_prompt(z, lg, lb, ws, bst):
    c = GMLP_CHUNK
    return pl.pallas_call(
        _gmlp_prompt_kernel,
        grid=(N_PROMPT // c,),
        in_specs=[
            pl.BlockSpec((c, GMLP_W), lambda i: (i, COL_GU // GMLP_W)),
            pl.BlockSpec((c, GMLP_W), lambda i: (i, COL_GV // GMLP_W)),
            pl.BlockSpec((1, GMLP_W), lambda i: (0, 0)),
            pl.BlockSpec((1, GMLP_W), lambda i: (0, 0)),
            pl.BlockSpec((GMLP_GROUPS, c, c), lambda i: (0, 0, 0)),
            pl.BlockSpec((c, GMLP_GROUPS), lambda i: (0, 0)),
        ],
        out_specs=pl.BlockSpec((c, GMLP_W), lambda i: (i, 0)),
        out_shape=jax.ShapeDtypeStruct((N_PROMPT, GMLP_W), BF16),
        compiler_params=_cparams(("parallel",)),
        name="gmlp_prompt",
    )(z, z, lg, lb, ws, bst)


def _gmlp_sample_kernel(gu_ref, gv_ref, lg_ref, lb_ref, wd_ref, bd_ref, o_ref, vn_ref):
    vn = _layernorm(_gelu(gv_ref[...]), lg_ref[...], lb_ref[...])
    vn_ref[...] = vn
    o_ref[...] = (_gelu(gu_ref[...]) * (vn * wd_ref[...] + bd_ref[...])).astype(o_ref.dtype)


def _gmlp_sample(z, lg, lb, wd, bd):
    rb = N_PROMPT // DEC_BATCH
    vec = pl.BlockSpec((1, GMLP_W), lambda i: (0, 0))
    return pl.pallas_call(
        _gmlp_sample_kernel,
        grid=(1,),
        in_specs=[
            pl.BlockSpec((DEC_BATCH, GMLP_W), lambda i: (rb, COL_GU // GMLP_W)),
            pl.BlockSpec((DEC_BATCH, GMLP_W), lambda i: (rb, COL_GV // GMLP_W)),
            vec, vec, vec, vec,
        ],
        out_specs=[pl.BlockSpec((DEC_BATCH, GMLP_W), lambda i: (0, 0))] * 2,
        out_shape=[jax.ShapeDtypeStruct((DEC_BATCH, GMLP_W), BF16), jax.ShapeDtypeStruct((DEC_BATCH, GMLP_W), F32)],
        compiler_params=_cparams(("arbitrary",)),
        name="gmlp_sample",
    )(z, z, lg, lb, wd, bd)


def _softmax_rows(s):
    e = jnp.exp(s - jnp.max(s, axis=-1, keepdims=True))
    return e / jnp.sum(e, axis=-1, keepdims=True)


def _xattn_prompt_kernel(q_ref, k_ref, v_ref, o_ref):
    s = _dot_nt(q_ref[...].astype(BF16), k_ref[...].astype(BF16)) * np.float32(XA_DH ** -0.5)
    o_ref[...] = _dot(_softmax_rows(s).astype(BF16), v_ref[...].astype(BF16)).astype(o_ref.dtype)


def _xattn_prompt(z, kv, tt=512):
    n_t = SEQ // tt
    return pl.pallas_call(
        _xattn_prompt_kernel,
        grid=(BATCH, XA_HEADS, n_t),
        in_specs=[
            pl.BlockSpec((tt, XA_DH), lambda b, h, t: (b * n_t + t, COL_XQ // XA_DH + h)),
            pl.BlockSpec((N_MEM, XA_DH), lambda b, h, t: (b, h)),
            pl.BlockSpec((N_MEM, XA_DH), lambda b, h, t: (b, XA_HEADS + h)),
        ],
        out_specs=pl.BlockSpec((tt, XA_DH), lambda b, h, t: (b * n_t + t, h)),
        out_shape=jax.ShapeDtypeStruct((N_PROMPT, XA_W), BF16),
        compiler_params=_cparams(("parallel", "parallel", "parallel")),
        name="xattn_prompt",
    )(z, kv, kv)


def _xattn_sample_kernel(q_ref, k_ref, v_ref, o_ref):
    b = pl.program_id(0)
    q = q_ref[pl.ds(b, 1), :]
    qh = jnp.concatenate([q[:, h * XA_DH:(h + 1) * XA_DH] for h in range(XA_HEADS)], axis=0)
    s = jnp.sum(k_ref[0, 0] * qh[None], axis=-1, keepdims=True) * np.float32(XA_DH ** -0.5)
    e = jnp.exp(s - jnp.max(s, axis=0, keepdims=True))
    p = e / jnp.sum(e, axis=0, keepdims=True)
    o = jnp.sum(p * v_ref[0, 0], axis=0)
    for h in range(XA_HEADS):
        o_ref[pl.ds(b, 1), h * XA_DH:(h + 1) * XA_DH] = o[h:h + 1, :]


def _xattn_sample(z, ck, cv):
    rb = N_PROMPT // DEC_BATCH
    cache_spec = pl.BlockSpec((1, 1, N_MEM, XA_HEADS, XA_DH), lambda b: (0, b, 0, 0, 0))
    return pl.pallas_call(
        _xattn_sample_kernel,
        grid=(DEC_BATCH,),
        in_specs=[
            pl.BlockSpec((DEC_BATCH, XA_W), lambda b: (rb, COL_XQ // XA_W)),
            cache_spec,
            cache_spec,
        ],
        out_specs=pl.BlockSpec((DEC_BATCH, XA_W), lambda b: (0, 0)),
        out_shape=jax.ShapeDtypeStruct((DEC_BATCH, XA_W), F32),
        compiler_params=_cparams(("arbitrary",)),
        name="xattn_sample",
    )(z, ck, cv)


def _merge_kernel(a_ref, b_ref, c_ref, g0_ref, g1_ref, g2_ref, bg0_ref, bg1_ref, bg2_ref, wa_ref, wb_ref, wc_ref,
                  o_ref):
    m = jax.nn.sigmoid(g0_ref[...] + bg0_ref[...]) * _dot(a_ref[...], wa_ref[...])
    m = m + jax.nn.sigmoid(g1_ref[...] + bg1_ref[...]) * _dot(b_ref[...], wb_ref[...])
    m = m + jax.nn.sigmoid(g2_ref[...] + bg2_ref[...]) * _dot(c_ref[...], wc_ref[...])
    o_ref[...] = m.astype(o_ref.dtype)


def _merge(br_a, br_b, br_c, z, bg, wa, wb, wc, tm=TOK_TILE, tn=512):
    gblk = COL_GATE // tn
    per = D_MODEL // tn

    def gate_spec(i):
        return pl.BlockSpec((tm, tn), lambda j, r, i=i: (r, gblk + i * per + j))

    def bias_spec(i):
        return pl.BlockSpec((1, tn), lambda j, r, i=i: (0, i * per + j))

    return pl.pallas_call(
        _merge_kernel,
        grid=(D_MODEL // tn, N_PAD // tm),
        in_specs=[
            pl.BlockSpec((tm, br_a.shape[1]), lambda j, r: (r, 0)),
            pl.BlockSpec((tm, br_b.shape[1]), lambda j, r: (r, 0)),
            pl.BlockSpec((tm, br_c.shape[1]), lambda j, r: (r, 0)),
            gate_spec(0), gate_spec(1), gate_spec(2),
            bias_spec(0), bias_spec(1), bias_spec(2),
            pl.BlockSpec((wa.shape[0], tn), lambda j, r: (0, j)),
            pl.BlockSpec((wb.shape[0], tn), lambda j, r: (0, j)),
            pl.BlockSpec((wc.shape[0], tn), lambda j, r: (0, j)),
        ],
        out_specs=pl.BlockSpec((tm, tn), lambda j, r: (r, j)),
        out_shape=jax.ShapeDtypeStruct((N_PAD, D_MODEL), BF16),
        compiler_params=_cparams(("parallel", "parallel")),
        name="merge",
    )(br_a, br_b, br_c, z, z, z, bg, bg, bg, wa, wb, wc)


def _resid_matmul_kernel(x_ref, m_ref, w_ref, o_ref):
    o_ref[...] = x_ref[...] + _dot(m_ref[...], w_ref[...])


def _resid_matmul(x, m, w, tm=TOK_TILE, tn=512):
    return pl.pallas_call(
        _resid_matmul_kernel,
        grid=(D_MODEL // tn, N_PAD // tm),
        in_specs=[
            pl.BlockSpec((tm, tn), lambda j, r: (r, j)),
            pl.BlockSpec((tm, D_MODEL), lambda j, r: (r, 0)),
            pl.BlockSpec((D_MODEL, tn), lambda j, r: (0, j)),
        ],
        out_specs=pl.BlockSpec((tm, tn), lambda j, r: (r, j)),
        out_shape=jax.ShapeDtypeStruct((N_PAD, D_MODEL), F32),
        compiler_params=_cparams(("parallel", "parallel")),
        name="resid_matmul",
    )(x, m, w)


def _sort_desc(a):
    a = list(a)
    n = len(a)
    k = 2
    while k <= n:
        j = k // 2
        while j >= 1:
            for i in range(n):
                l = i ^ j
                if l > i:
                    hi, lo = jnp.maximum(a[i], a[l]), jnp.minimum(a[i], a[l])
                    a[i], a[l] = (hi, lo) if (i & k) == 0 else (lo, hi)
            j //= 2
        k *= 2
    return a


def _merge_bitonic_desc(c):
    c = list(c)
    n = len(c)
    j = n // 2
    while j >= 1:
        for i in range(n):
            l = i ^ j
            if l > i:
                c[i], c[l] = jnp.maximum(c[i], c[l]), jnp.minimum(c[i], c[l])
        j //= 2
    return c


def _merge_top(a, b):
    n = len(a)
    return _merge_bitonic_desc([jnp.maximum(a[i], b[n - 1 - i]) for i in range(n)])


def _merge_sublanes(a):
    for shift in (4, 2, 1):
        a = _merge_top(a, [pltpu.roll(v, shift, axis=0) for v in a])
    return a


def _top16_of_keys(x):
    return _merge_sublanes(_sort_desc([x[8 * i:8 * i + 8, :] for i in range(N_KEYS // 8)]))


def _peer_route_kernel(q_ref, k1_ref, k2_ref, s1_ref, s2_ref, e1_ref, e2_ref, tau_ref):
    q = q_ref[...]
    s1_ref[0] = _dot_nt(k1_ref[0].astype(BF16), q[:, :PEER_HALF].astype(BF16))
    s2_ref[0] = _dot_nt(k2_ref[0].astype(BF16), q[:, PEER_HALF:].astype(BF16))
    sub = lax.broadcasted_iota(jnp.int32, (8, LANE), 0)
    for t in range(q.shape[0] // LANE):
        lanes = slice(t * LANE, (t + 1) * LANE)
        s1 = s1_ref[0, :, lanes]
        s2 = s2_ref[0, :, lanes]
        v1 = _top16_of_keys(s1)
        v2 = _top16_of_keys(s2)
        lo, hi = v1[0], v1[8]
        for r in range(1, 8):
            lo = jnp.where(sub == r, v1[r], lo)
            hi = jnp.where(sub == r, v1[8 + r], hi)
        cand = [lo + v2[j] for j in range(PEER_TOPK)]
        cand[PEER_TOPK - 1] = jnp.maximum(cand[PEER_TOPK - 1], hi + v2[0])
        top = _merge_sublanes(_merge_bitonic_desc(cand))
        zsum = jnp.exp(top[0] - top[0])
        for c in top[1:]:
            zsum = zsum + jnp.exp(c - top[0])
        inv = 1.0 / zsum
        for i in range(N_KEYS // 8):
            rows = slice(8 * i, 8 * i + 8)
            e1_ref[0, rows, lanes] = jnp.exp(s1[rows, :] - v1[0]) * inv
            e2_ref[0, rows, lanes] = jnp.exp(s2[rows, :] - v2[0])
        tau_ref[0, :, lanes] = top[PEER_TOPK - 1][0:1, :]


def _peer_route(q, k1, k2, tm=TOK_TILE):
    sspec = pl.BlockSpec((1, N_KEYS, tm), lambda h, i: (h, 0, i))
    kspec = pl.BlockSpec((1, N_KEYS, PEER_HALF), lambda h, i: (h, 0, 0))
    sshape = jax.ShapeDtypeStruct((PEER_HEADS, N_KEYS, N_PAD), F32)
    return pl.pallas_call(
        _peer_route_kernel,
        grid=(PEER_HEADS, N_PAD // tm),
        in_specs=[pl.BlockSpec((tm, 2 * PEER_HALF), lambda h, i: (i, h)), kspec, kspec],
        out_specs=[sspec, sspec, sspec, sspec, pl.BlockSpec((1, 1, tm), lambda h, i: (h, 0, i))],
        out_shape=[sshape, sshape, sshape, sshape, jax.ShapeDtypeStruct((PEER_HEADS, 1, N_PAD), F32)],
        compiler_params=_cparams(("parallel", "parallel")),
        name="peer_route",
    )(q, k1, k2)


PEER_E1_PER_BLOCK = 8
PEER_EXPERT_BLOCK = PEER_E1_PER_BLOCK * N_KEYS
PEER_E1_GROUP = 4
PEER_E2_SPLIT = 4


def _peer_dense_kernel(ht_ref, u_ref, vt_ref, s1_ref, e1_ref, s2_ref, e2_ref, tau_ref, o_ref, act_scr, coef_ref):
    j = pl.program_id(1)
    e2_rows = N_KEYS // PEER_E2_SPLIT

    @pl.when(j == 0)
    def _():
        o_ref[...] = jnp.zeros(o_ref.shape, o_ref.dtype)

    act_scr[...] = _dot(u_ref[...], ht_ref[...])
    for t in range(ht_ref.shape[1] // LANE):
        lanes = slice(t * LANE, (t + 1) * LANE)
        for part in range(PEER_E2_SPLIT):
            e2s = slice(part * e2_rows, (part + 1) * e2_rows)
            for grp in range(PEER_E1_PER_BLOCK // PEER_E1_GROUP):
                acc = [None] * PEER_E1_GROUP
                for h in range(PEER_HEADS):
                    s2 = s2_ref[h, e2s, lanes]
                    e2 = e2_ref[h, e2s, lanes]
                    tau = tau_ref[h, :, lanes]
                    for l in range(PEER_E1_GROUP):
                        r = grp * PEER_E1_GROUP + l
                        pair = s1_ref[h, r:r + 1, lanes] + s2
                        w = jnp.where(pair >= tau, e2 * e1_ref[h, r:r + 1, lanes], 0.0)
                        acc[l] = w if acc[l] is None else acc[l] + w
                for l in range(PEER_E1_GROUP):
                    r = grp * PEER_E1_GROUP + l
                    rows = slice(r * N_KEYS + part * e2_rows, r * N_KEYS + (part + 1) * e2_rows)
                    coef_ref[rows, lanes] = (acc[l] * _gelu(act_scr[rows, lanes])).astype(BF16)
    o_ref[...] += _dot(vt_ref[...], coef_ref[...])


def _peer_dense(ht, u, vt, s1t, e1t, s2t, e2t, tau, tn=TOK_TILE):
    eb = PEER_EXPERT_BLOCK
    n_exp = u.shape[0]
    rowspec = pl.BlockSpec((PEER_HEADS, PEER_E1_PER_BLOCK, tn), lambda i, j: (0, j, i))
    slabspec = pl.BlockSpec((PEER_HEADS, N_KEYS, tn), lambda i, j: (0, 0, i))
    return pl.pallas_call(
        _peer_dense_kernel,
        grid=(N_PAD // tn, n_exp // eb),
        in_specs=[
            pl.BlockSpec((D_MODEL, tn), lambda i, j: (0, i)),
            pl.BlockSpec((eb, D_MODEL), lambda i, j: (j, 0)),
            pl.BlockSpec((D_MODEL, eb), lambda i, j: (0, j)),
            rowspec, rowspec, slabspec, slabspec,
            pl.BlockSpec((PEER_HEADS, 1, tn), lambda i, j: (0, 0, i)),
        ],
        out_specs=pl.BlockSpec((D_MODEL, tn), lambda i, j: (0, i)),
        out_shape=jax.ShapeDtypeStruct((D_MODEL, N_PAD), F32),
        scratch_shapes=[pltpu.VMEM((eb, tn), F32), pltpu.VMEM((eb, tn), BF16)],
        compiler_params=pltpu.CompilerParams(dimension_semantics=("parallel", "arbitrary"),
                                             vmem_limit_bytes=PEER_VMEM_LIMIT),
        name="peer_dense",
    )(ht, u, vt, s1t, e1t, s2t, e2t, tau)


def _final_norm_kernel(x_ref, pt_ref, g_ref, o_ref):
    o_ref[...] = _rms(x_ref[...] + pt_ref[...].T, g_ref[...])


def _final_norm(x, pt, g, row0, n_rows, tm):
    b0 = row0 // tm
    return pl.pallas_call(
        _final_norm_kernel,
        grid=(n_rows // tm,),
        in_specs=[
            pl.BlockSpec((tm, D_MODEL), lambda i: (b0 + i, 0)),
            pl.BlockSpec((D_MODEL, tm), lambda i: (0, b0 + i)),
            pl.BlockSpec((1, D_MODEL), lambda i: (0, 0)),
        ],
        out_specs=pl.BlockSpec((tm, D_MODEL), lambda i: (i, 0)),
        out_shape=jax.ShapeDtypeStruct((n_rows, D_MODEL), F32),
        compiler_params=_cparams(("parallel",)),
        name="final_norm",
    )(x, pt, g)


def kernel(x_prompt, x_sample, mem_prompt, state_gla, cache_mem_k, cache_mem_v, norm_mix_g, w_in, b_gate, w_alpha_up, b_alpha, gla_norm_g, gmlp_ln_g, gmlp_ln_b, gmlp_w_s, gmlp_b_s, mem_norm_g, w_mem_kv, w_br_gla, w_br_gmlp, w_br_xattn, w_out, norm_ffn_g, peer_w_q, peer_sub_k1, peer_sub_k2, peer_u, peer_v, norm_final_g):
    assert w_in.shape[0] == 1, "single layer"
    x = jnp.concatenate([x_prompt.reshape(N_PROMPT, D_MODEL), x_sample.reshape(DEC_BATCH, D_MODEL),
                         jnp.zeros((N_PAD - N_TOK, D_MODEL), F32)], axis=0)

    wi = w_in[0]
    w_head = wi[:, :Z_HEAD_COLS].astype(BF16)
    w_tail = wi[:, Z_TAIL_START:].astype(BF16)
    w_alow = jnp.pad(wi[:, Z_HEAD_COLS:Z_TAIL_START], ((0, 0), (0, ALOW_PAD - GLA_RANK))).astype(BF16)
    z, za = _in_proj(x, norm_mix_g[0].reshape(1, D_MODEL), w_head, w_tail, w_alow)

    wup = jnp.pad(w_alpha_up[0], ((0, ALOW_PAD - GLA_RANK), (0, 0))).astype(BF16)
    ba = b_alpha[0].reshape(1, -1)
    gn = gla_norm_g[0].reshape(1, -1)
    br_a_p, gla_state_p = _gla_prompt(z, za, wup, ba, gn)
    br_a_s, gla_state_s = _gla_sample(z, za, wup, ba, gn, state_gla[0])

    lg = gmlp_ln_g[0].reshape(1, -1)
    lb = gmlp_ln_b[0].reshape(1, -1)
    br_b_p = _gmlp_prompt(z, lg, lb, gmlp_w_s[0], gmlp_b_s[0].T)
    wd = jnp.repeat(gmlp_w_s[0][:, 0, 0], GMLP_DG).reshape(1, -1)
    bd = jnp.repeat(gmlp_b_s[0][:, 0], GMLP_DG).reshape(1, -1)
    br_b_s, vn_s = _gmlp_sample(z, lg, lb, wd, bd)

    kv = _norm_matmul(mem_prompt.reshape(BATCH * N_MEM, D_MODEL), mem_norm_g[0].reshape(1, -1),
                      w_mem_kv[0].astype(BF16), 512, 1024)
    br_c_p = _xattn_prompt(z, kv)
    br_c_s = _xattn_sample(z, cache_mem_k, cache_mem_v)

    def rows(prompt, sample):
        pad = jnp.zeros((N_PAD - N_TOK, prompt.shape[1]), BF16)
        return jnp.concatenate([prompt, sample.astype(BF16), pad], axis=0)

    br_a, br_b, br_c = rows(br_a_p, br_a_s), rows(br_b_p, br_b_s), rows(br_c_p, br_c_s)
    merged = _merge(br_a, br_b, br_c, z, b_gate[0].reshape(1, -1), w_br_gla[0].astype(BF16),
                    w_br_gmlp[0].astype(BF16), w_br_xattn[0].astype(BF16))
    x1 = _resid_matmul(x, merged, w_out[0].astype(BF16))

    q, h2t = _norm_matmul(x1, norm_ffn_g[0].reshape(1, -1), peer_w_q[0].astype(BF16), TOK_TILE, 1024, with_h=True)
    s1t, s2t, e1t, e2t, tau = _peer_route(q, peer_sub_k1[0], peer_sub_k2[0])
    peer_t = _peer_dense(h2t, peer_u[0].astype(BF16), peer_v[0].T.astype(BF16), s1t, e1t, s2t, e2t, tau)
    g_fin = norm_final_g.reshape(1, -1)
    y_prompt = _final_norm(x1, peer_t, g_fin, 0, N_PROMPT, 512).reshape(BATCH, SEQ, D_MODEL)
    y_sample = _final_norm(x1, peer_t, g_fin, N_PROMPT, DEC_BATCH, DEC_BATCH).reshape(DEC_BATCH, 1, D_MODEL)
    mem_k = kv[:, :XA_W].reshape(1, BATCH, N_MEM, XA_HEADS, XA_DH)
    mem_v = kv[:, XA_W:].reshape(1, BATCH, N_MEM, XA_HEADS, XA_DH)
    return (y_prompt, y_sample, gla_state_p[None], gla_state_s[None], mem_k, mem_v,
            vn_s.reshape(1, DEC_BATCH, 1, GMLP_W))
```
